```python
import jax, jax.numpy as jnp
from jax import lax
import numpy as np

D_MODEL = 1024
BATCH = 8
SEQ = 2048
DEPTH = 1
DEC_BATCH = 32
DEC_SEQ = 1
PAST_LEN = 8192
PAGE_SIZE = 128

N_HEADS = 8
HEAD_DIM = 128
N_KV_HEADS = 2
Q_GROUP = N_HEADS // N_KV_HEADS
ATTN_WIDTH = N_HEADS * HEAD_DIM
IDX_HEADS = 16
IDX_DIM = 64
TOPK_MAX = 256
Q_BLOCK = 128
D_INNER = 2 * D_MODEL
SSM_HEAD_DIM = 64
SSM_HEADS = D_INNER // SSM_HEAD_DIM
SSM_GROUPS = 8
HEADS_PER_GROUP = SSM_HEADS // SSM_GROUPS
D_STATE = 128
CONV_W = 4
CONV_DIM = D_INNER + 2 * SSM_GROUPS * D_STATE
CHUNK = 128
D_FF = -(-(-(-8 * D_MODEL // 3)) // 256) * 256
EPS = 1e-6

SPLIT_SIZES = (ATTN_WIDTH, N_KV_HEADS * HEAD_DIM, N_KV_HEADS * HEAD_DIM,
               IDX_HEADS * IDX_DIM, IDX_DIM, IDX_HEADS,
               D_INNER, CONV_DIM, SSM_HEADS, D_MODEL, D_MODEL)
D_IN_PROJ = (ATTN_WIDTH + 2 * N_KV_HEADS * HEAD_DIM + IDX_HEADS * IDX_DIM + IDX_DIM + IDX_HEADS
             + D_INNER + CONV_DIM + SSM_HEADS + 2 * D_MODEL)

kernel_name = "hybrid_dsa_ssd_gated_decoder_step"

F32 = jnp.float32


def rmsnorm(x, w):
    xf = x.astype(F32)
    y = xf * lax.rsqrt(jnp.mean(xf * xf, axis=-1, keepdims=True) + EPS)
    return (y * w.astype(F32)).astype(x.dtype)


def layernorm(x, w, b):
    xf = x.astype(F32)
    mu = jnp.mean(xf, axis=-1, keepdims=True)
    var = jnp.mean(jnp.square(xf - mu), axis=-1, keepdims=True)
    y = (xf - mu) * lax.rsqrt(var + EPS)
    return (y * w.astype(F32) + b.astype(F32)).astype(x.dtype)


def topk_count(n_keys):
    return min(TOPK_MAX, n_keys // 4)


def in_projection(x, norm1_w, w_in, idx_ln_w, idx_ln_b):
    b, t = x.shape[:2]
    u = rmsnorm(x, norm1_w) @ w_in
    offsets = [int(o) for o in np.cumsum(SPLIT_SIZES)[:-1]]
    q, k, v, iq, ik, iw, z, xbc, dt, ga, gm = jnp.split(u, offsets, axis=-1)
    q = q.reshape(b, t, N_HEADS, HEAD_DIM)
    k = k.reshape(b, t, N_KV_HEADS, HEAD_DIM)
    v = v.reshape(b, t, N_KV_HEADS, HEAD_DIM)
    iq = iq.reshape(b, t, IDX_HEADS, IDX_DIM)
    ik = layernorm(ik, idx_ln_w, idx_ln_b)
    iw = iw * (IDX_HEADS ** -0.5)
    return q, k, v, iq, ik, iw, z, xbc, dt, ga, gm


def index_scores(iq, iw, ik):
    s = jax.nn.relu(jnp.einsum('bthd,bsd->bths', iq.astype(F32), ik.astype(F32))) * (IDX_DIM ** -0.5)
    return jnp.einsum('bths,bth->bts', s, iw.astype(F32))


def select_keys(scores, q_pos, n_sel):
    key_pos = jnp.arange(scores.shape[-1])
    admissible = key_pos[None, None, :] <= q_pos[None, :, None]
    _, idx = lax.top_k(jnp.where(admissible, scores, -jnp.inf), n_sel)
    valid = idx <= q_pos[None, :, None]
    return idx, valid


def attend_selected(q, k_sel, v_sel, valid):
    b, t = q.shape[:2]
    qg = q.reshape(b, t, N_KV_HEADS, Q_GROUP, HEAD_DIM).astype(F32)
    logits = jnp.einsum('btkgd,btjkd->btkgj', qg, k_sel.astype(F32)) * (HEAD_DIM ** -0.5)
    logits = jnp.where(valid[:, :, None, None, :], logits, -jnp.inf)
    p = jax.nn.softmax(logits, axis=-1)
    o = jnp.einsum('btkgj,btjkd->btkgd', p, v_sel.astype(F32))
    return o.reshape(b, t, ATTN_WIDTH).astype(q.dtype)


def prompt_sparse_attention(q, k, v, iq, iw, ik):
    b, s = q.shape[:2]
    n_blk = s // Q_BLOCK
    n_sel = topk_count(s)
    bidx = jnp.arange(b)[:, None, None]

    def to_blocks(a):
        return a.reshape(b, n_blk, Q_BLOCK, *a.shape[2:]).swapaxes(0, 1)

    def block(args):
        blk, qb, iqb, iwb = args
        q_pos = blk * Q_BLOCK + jnp.arange(Q_BLOCK)
        idx, valid = select_keys(index_scores(iqb, iwb, ik), q_pos, n_sel)
        return attend_selected(qb, k[bidx, idx], v[bidx, idx], valid)

    out = lax.map(block, (jnp.arange(n_blk), to_blocks(q), to_blocks(iq), to_blocks(iw)))
    return out.swapaxes(0, 1).reshape(b, s, ATTN_WIDTH)


def sample_sparse_attention(q, k_new, v_new, iq, iw, ik_new, cache_k, cache_v, cache_idx_k, page_table):
    db, ds = q.shape[:2]
    past = page_table.shape[1] * PAGE_SIZE
    ik_past = cache_idx_k[page_table].reshape(db, past, IDX_DIM).astype(ik_new.dtype)
    ik_all = jnp.concatenate([ik_past, ik_new], axis=1)
    q_pos = past + jnp.arange(ds)
    idx, valid = select_keys(index_scores(iq, iw, ik_all), q_pos, topk_count(past + ds))
    bidx = jnp.arange(db)[:, None, None]
    in_past = (idx < past)[..., None, None]
    pidx = jnp.minimum(idx, past - 1)
    phys = page_table[bidx, pidx // PAGE_SIZE]
    off = pidx % PAGE_SIZE
    nidx = jnp.clip(idx - past, 0, ds - 1)
    k_sel = jnp.where(in_past, cache_k[phys, off].astype(k_new.dtype), k_new[bidx, nidx])
    v_sel = jnp.where(in_past, cache_v[phys, off].astype(v_new.dtype), v_new[bidx, nidx])
    return attend_selected(q, k_sel, v_sel, valid)


def causal_conv(xpad, conv_w, conv_b):
    t = xpad.shape[1] - (CONV_W - 1)
    out = conv_b
    for i in range(CONV_W):
        out = out + xpad[:, i:i + t] * conv_w[i]
    return jax.nn.silu(out)


def ssd_chunked(xs, Bm, Cm, dt, A):
    b, t = xs.shape[:2]
    nc = t // CHUNK
    c = lambda a: a.reshape(b, nc, CHUNK, *a.shape[2:])
    x_c, B_c, C_c, dt_c = c(xs), c(Bm), c(Cm), c(dt)
    xdt = x_c * dt_c[..., None]
    a_cum = jnp.cumsum(dt_c * A, axis=2)
    seg = a_cum[:, :, :, None] - a_cum[:, :, None, :]
    tri = jnp.arange(CHUNK)[:, None] >= jnp.arange(CHUNK)[None, :]
    L = jnp.exp(jnp.where(tri[:, :, None, None], seg, -jnp.inf))
    CB = jnp.einsum('bclgn,bcsgn->bclsg', C_c, B_c)
    y_diag = jnp.einsum('bclsg,bclsgr,bcsgrp->bclgrp', CB, L, xdt)

    def chunk_step(h, inp):
        Bk, Ck, xk, ak = inp
        y_off = jnp.einsum('blgn,bgrpn,blgr->blgrp', Ck, h, jnp.exp(ak))
        h_new = (h * jnp.exp(ak[:, -1])[..., None, None]
                 + jnp.einsum('blgn,blgr,blgrp->bgrpn', Bk, jnp.exp(ak[:, -1:] - ak), xk))
        return h_new, y_off

    h0 = jnp.zeros((b, SSM_GROUPS, HEADS_PER_GROUP, SSM_HEAD_DIM, D_STATE), F32)
    sw = lambda a: a.swapaxes(0, 1)
    h_fin, y_off = lax.scan(chunk_step, h0, (sw(B_c), sw(C_c), sw(xdt), sw(a_cum)))
    y = (y_diag + sw(y_off)).reshape(b, t, SSM_GROUPS, HEADS_PER_GROUP, SSM_HEAD_DIM)
    return y, h_fin


def ssd_recurrent(xs, Bm, Cm, dt, A, h0):
    def step(h, inp):
        xt, Bt, Ct, dtt = inp
        h = h * jnp.exp(dtt * A)[..., None, None] + jnp.einsum('bgrp,bgn,bgr->bgrpn', xt, Bt, dtt)
        return h, jnp.einsum('bgrpn,bgn->bgrp', h, Ct)

    sw = lambda a: a.swapaxes(0, 1)
    h, ys = lax.scan(step, h0, (sw(xs), sw(Bm), sw(Cm), sw(dt)))
    return sw(ys), h


def ssd_branch(z, xbc_pad, dt_raw, h0, conv_w, conv_b, dt_bias, a_log, d_skip, ssm_norm_w):
    b, t = z.shape[:2]
    xbc = causal_conv(xbc_pad, conv_w, conv_b).astype(F32)
    xs, Bm, Cm = jnp.split(xbc, [D_INNER, D_INNER + SSM_GROUPS * D_STATE], axis=-1)
    xs = xs.reshape(b, t, SSM_GROUPS, HEADS_PER_GROUP, SSM_HEAD_DIM)
    Bm = Bm.reshape(b, t, SSM_GROUPS, D_STATE)
    Cm = Cm.reshape(b, t, SSM_GROUPS, D_STATE)
    dt = jax.nn.softplus(dt_raw.astype(F32) + dt_bias.astype(F32)).reshape(b, t, SSM_GROUPS, HEADS_PER_GROUP)
    A = -jnp.exp(a_log.astype(F32)).reshape(SSM_GROUPS, HEADS_PER_GROUP)
    if h0 is None:
        y, h = ssd_chunked(xs, Bm, Cm, dt, A)
    else:
        h_in = h0.astype(F32).reshape(b, SSM_GROUPS, HEADS_PER_GROUP, SSM_HEAD_DIM, D_STATE)
        y, h = ssd_recurrent(xs, Bm, Cm, dt, A, h_in)
    y = y + d_skip.astype(F32).reshape(SSM_GROUPS, HEADS_PER_GROUP)[..., None] * xs
    y = y.reshape(b, t, D_INNER) * jax.nn.silu(z.astype(F32))
    yg = y.reshape(b, t, SSM_GROUPS, D_INNER // SSM_GROUPS)
    yg = yg * lax.rsqrt(jnp.mean(yg * yg, axis=-1, keepdims=True) + EPS)
    y = yg.reshape(b, t, D_INNER) * ssm_norm_w.astype(F32)
    return y.astype(z.dtype), h.reshape(b, SSM_HEADS, SSM_HEAD_DIM, D_STATE).astype(z.dtype)


def merge_and_ffn(x, attn_o, ssm_o, ga, gm, w_attn_br, w_ssm_br, w_out, norm2_w, w_ffn_gate, w_ffn_up, w_ffn_down):
    mix = jax.nn.sigmoid(ga) * (attn_o @ w_attn_br) + jax.nn.sigmoid(gm) * (ssm_o @ w_ssm_br)
    x = x + mix @ w_out
    h = rmsnorm(x, norm2_w)
    return x + (jax.nn.silu(h @ w_ffn_gate) * (h @ w_ffn_up)) @ w_ffn_down


def setup_inputs(seed: int = 0) -> dict:
    key = jax.random.key(seed)
    ks = jax.random.split(key, 32)
    n_pages = PAST_LEN // PAGE_SIZE
    n_used = DEC_BATCH * n_pages
    n_phys = n_used + -(-n_used // 4)
    nrm = lambda k, shape, scale=1.0: jax.random.normal(k, shape, F32) * scale
    page_table = jax.random.permutation(ks[0], n_phys)[:n_used].reshape(DEC_BATCH, n_pages).astype(jnp.int32)
    dt0 = jnp.exp(jax.random.uniform(ks[1], (DEPTH, SSM_HEADS), F32, np.log(1e-3), np.log(1e-1)))
    return {
        "x_prompt": nrm(ks[2], (BATCH, SEQ, D_MODEL)),
        "x_sample": nrm(ks[3], (DEC_BATCH, DEC_SEQ, D_MODEL)),
        "cache_k": nrm(ks[4], (DEPTH, n_phys, PAGE_SIZE, N_KV_HEADS, HEAD_DIM)),
        "cache_v": nrm(ks[5], (DEPTH, n_phys, PAGE_SIZE, N_KV_HEADS, HEAD_DIM)),
        "cache_idx_k": nrm(ks[6], (DEPTH, n_phys, PAGE_SIZE, IDX_DIM)),
        "state_ssm": nrm(ks[7], (DEPTH, DEC_BATCH, SSM_HEADS, SSM_HEAD_DIM, D_STATE), 0.5),
        "state_conv": nrm(ks[8], (DEPTH, DEC_BATCH, CONV_W - 1, CONV_DIM)),
        "page_table": page_table,
        "norm1_w": 1.0 + nrm(ks[9], (DEPTH, D_MODEL), 0.05),
        "w_in": nrm(ks[10], (DEPTH, D_MODEL, D_IN_PROJ), D_MODEL ** -0.5),
        "idx_ln_w": 1.0 + nrm(ks[11], (DEPTH, IDX_DIM), 0.05),
        "idx_ln_b": nrm(ks[12], (DEPTH, IDX_DIM), 0.02),
        "conv_w": nrm(ks[13], (DEPTH, CONV_W, CONV_DIM), CONV_W ** -0.5),
        "conv_b": nrm(ks[14], (DEPTH, CONV_DIM), 0.02),
        "dt_bias": dt0 + jnp.log(-jnp.expm1(-dt0)),
        "a_log": jnp.log(jax.random.uniform(ks[15], (DEPTH, SSM_HEADS), F32, 1.0, 16.0)),
        "d_skip": 1.0 + nrm(ks[16], (DEPTH, SSM_HEADS), 0.1),
        "ssm_norm_w": 1.0 + nrm(ks[17], (DEPTH, D_INNER), 0.05),
        "w_attn_br": nrm(ks[18], (DEPTH, ATTN_WIDTH, D_MODEL), ATTN_WIDTH ** -0.5),
        "w_ssm_br": nrm(ks[19], (DEPTH, D_INNER, D_MODEL), D_INNER ** -0.5),
        "w_out": nrm(ks[20], (DEPTH, D_MODEL, D_MODEL), D_MODEL ** -0.5),
        "norm2_w": 1.0 + nrm(ks[21], (DEPTH, D_MODEL), 0.05),
        "w_ffn_gate": nrm(ks[22], (DEPTH, D_MODEL, D_FF), D_MODEL ** -0.5),
        "w_ffn_up": nrm(ks[23], (DEPTH, D_MODEL, D_FF), D_MODEL ** -0.5),
        "w_ffn_down": nrm(ks[24], (DEPTH, D_FF, D_MODEL), D_FF ** -0.5),
        "normf_w": 1.0 + nrm(ks[25], (D_MODEL,), 0.05),
    }


def reference(x_prompt, x_sample, cache_k, cache_v, cache_idx_k, state_ssm, state_conv, page_table,
              norm1_w, w_in, idx_ln_w, idx_ln_b, conv_w, conv_b, dt_bias, a_log, d_skip, ssm_norm_w,
              w_attn_br, w_ssm_br, w_out, norm2_w, w_ffn_gate, w_ffn_up, w_ffn_down, normf_w):
    xp, xs = x_prompt, x_sample
    kp_l, vp_l, ikp_l, hp_l, cp_l = [], [], [], [], []
    ks_l, vs_l, iks_l, hs_l, cs_l = [], [], [], [], []
    for l in range(DEPTH):
        ssm_w = (conv_w[l], conv_b[l], dt_bias[l], a_log[l], d_skip[l], ssm_norm_w[l])
        tail_w = (w_attn_br[l], w_ssm_br[l], w_out[l], norm2_w[l], w_ffn_gate[l], w_ffn_up[l], w_ffn_down[l])
        q, k, v, iq, ik, iw, z, xbc, dt, ga, gm = in_projection(xp, norm1_w[l], w_in[l], idx_ln_w[l], idx_ln_b[l])
        a_o = prompt_sparse_attention(q, k, v, iq, iw, ik)
        xbc_pad = jnp.pad(xbc, ((0, 0), (CONV_W - 1, 0), (0, 0)))
        m_o, h_fin = ssd_branch(z, xbc_pad, dt, None, *ssm_w)
        xp = merge_and_ffn(xp, a_o, m_o, ga, gm, *tail_w)
        kp_l.append(k); vp_l.append(v); ikp_l.append(ik); hp_l.append(h_fin)
        cp_l.append(xbc[:, -(CONV_W - 1):])
        q, k, v, iq, ik, iw, z, xbc, dt, ga, gm = in_projection(xs, norm1_w[l], w_in[l], idx_ln_w[l], idx_ln_b[l])
        a_o = sample_sparse_attention(q, k, v, iq, iw, ik, cache_k[l], cache_v[l], cache_idx_k[l], page_table)
        xbc_pad = jnp.concatenate([state_conv[l].astype(xbc.dtype), xbc], axis=1)
        m_o, h_new = ssd_branch(z, xbc_pad, dt, state_ssm[l], *ssm_w)
        xs = merge_and_ffn(xs, a_o, m_o, ga, gm, *tail_w)
        ks_l.append(k); vs_l.append(v); iks_l.append(ik); hs_l.append(h_new)
        cs_l.append(xbc_pad[:, -(CONV_W - 1):])
    y_prompt = rmsnorm(xp, normf_w)
    y_sample = rmsnorm(xs, normf_w)
    return (y_prompt, y_sample,
            jnp.stack(kp_l), jnp.stack(vp_l), jnp.stack(ikp_l), jnp.stack(hp_l), jnp.stack(cp_l),
            jnp.stack(ks_l), jnp.stack(vs_l), jnp.stack(iks_l), jnp.stack(hs_l), jnp.stack(cs_l))
```

```python
import functools

import numpy as np
import jax
import jax.numpy as jnp
from jax import lax
from jax.experimental import pallas as pl
from jax.experimental.pallas import tpu as pltpu

F32, BF16, I32 = jnp.float32, jnp.bfloat16, jnp.int32

D_MODEL = 1024
PAGE_SIZE = 128
N_HEADS = 8
HEAD_DIM = 128
N_KV_HEADS = 2
Q_GROUP = N_HEADS // N_KV_HEADS
ATTN_WIDTH = N_HEADS * HEAD_DIM
KV_WIDTH = N_KV_HEADS * HEAD_DIM
IDX_HEADS = 16
IDX_DIM = 64
IDX_WIDTH = IDX_HEADS * IDX_DIM
TOPK_MAX = 256
Q_BLOCK = 128
D_INNER = 2 * D_MODEL
SSM_HEAD_DIM = 64
SSM_HEADS = D_INNER // SSM_HEAD_DIM
SSM_GROUPS = 8
HEADS_PER_GROUP = SSM_HEADS // SSM_GROUPS
GROUP_WIDTH = D_INNER // SSM_GROUPS
D_STATE = 128
CONV_W = 4
CONV_DIM = D_INNER + 2 * SSM_GROUPS * D_STATE
CHUNK = 128
D_FF = 2816
EPS = 1e-6

LANES = 128
SUBLANES = 8
VMEM_LIMIT_BYTES = 56 * 1024 * 1024

SPLIT_SIZES = (ATTN_WIDTH, KV_WIDTH, KV_WIDTH, IDX_WIDTH, IDX_DIM, IDX_HEADS,
               D_INNER, CONV_DIM, SSM_HEADS, D_MODEL, D_MODEL)
_OFFS = [0] + [int(o) for o in np.cumsum(SPLIT_SIZES)]
(O_Q, O_K, O_V, O_IQ, O_IK, O_IW, O_Z, O_XBC, O_DT, O_GA, O_GM, O_END) = _OFFS

INT_MIN = np.int32(-2 ** 31)
NEG_BIG = -1e30
IDX_SCALE = (IDX_DIM ** -0.5) * (IDX_HEADS ** -0.5)
ATTN_SCALE = HEAD_DIM ** -0.5

_NT = (((1,), (1,)), ((), ()))
_TN = (((0,), (0,)), ((), ()))


def _dot(a, b):
    return jnp.dot(a, b, preferred_element_type=F32)


def _dot_nt(a, b):
    return lax.dot_general(a, b, _NT, preferred_element_type=F32)


def _dot_tn(a, b):
    return lax.dot_general(a, b, _TN, preferred_element_type=F32)


def _rms(x, w):
    return (x * lax.rsqrt(jnp.mean(x * x, axis=-1, keepdims=True) + EPS)) * w


def _sigmoid(x):
    return 1.0 / (1.0 + jnp.exp(-x))


def _silu(x):
    return x * _sigmoid(x)


def _softplus(x):
    return jnp.maximum(x, 0.0) + jnp.log1p(jnp.exp(-jnp.abs(x)))


def _split3(x):
    hi = x.astype(BF16)
    r1 = x - hi.astype(F32)
    mid = r1.astype(BF16)
    lo = (r1 - mid.astype(F32)).astype(BF16)
    return hi, mid, lo


def _dot3(x, m):
    hi, mid, lo = _split3(x)
    return (_dot(hi, m) + _dot(mid, m)) + _dot(lo, m)


def _dot3_left(m, x):
    hi, mid, lo = _split3(x)
    return (_dot(m, hi) + _dot(m, mid)) + _dot(m, lo)


def _order_key(score):
    bits = lax.bitcast_convert_type(score, I32)
    return jnp.where(bits < 0, jnp.bitwise_xor(bits, np.int32(0x7FFFFFFF)), bits)


def _full(shape):
    nd = len(shape)
    return pl.BlockSpec(shape, lambda *_: (0,) * nd)


def _params(*sem):
    return pltpu.CompilerParams(dimension_semantics=sem, vmem_limit_bytes=VMEM_LIMIT_BYTES)


def _layernorm_rows(x, w, b):
    mu = jnp.mean(x, axis=-1, keepdims=True)
    var = jnp.mean(jnp.square(x - mu), axis=-1, keepdims=True)
    return ((x - mu) * lax.rsqrt(var + EPS)) * w + b


def _inproj_attn_t_kernel(x_ref, n1_ref, wqt_ref, wiqt_ref, wiwt_ref, wvt_ref, wk_ref, wv_ref,
                          wik_ref, lnw_ref, lnb_ref,
                          qt_ref, iqt_ref, iwt_ref, vt_ref, k_ref, v_ref, ik_ref):
    xn = _rms(x_ref[...], n1_ref[...]).astype(BF16)
    qt_ref[...] = _dot_nt(wqt_ref[...], xn).astype(BF16)
    iqt_ref[...] = _dot_nt(wiqt_ref[...], xn).astype(BF16)
    iwt_ref[...] = _dot_nt(wiwt_ref[...], xn) * IDX_SCALE
    vt = _dot_nt(wvt_ref[...], xn).astype(BF16)
    for j in range(vt_ref.shape[0]):
        vt_ref[j] = vt[:, j * Q_BLOCK:(j + 1) * Q_BLOCK]
    k_ref[...] = _dot(xn, wk_ref[...])
    v_ref[...] = _dot(xn, wv_ref[...])
    ik_ref[...] = _layernorm_rows(_dot(xn, wik_ref[...]), lnw_ref[...], lnb_ref[...])


def _inproj_attn_t(x, n1, wqt, wiqt, wiwt, wvt, wk, wv, wik, lnw, lnb, tm):
    m = x.shape[0]
    grid = (m // tm,)
    row = lambda w: pl.BlockSpec((tm, w), lambda i: (i, 0))
    col = lambda h: pl.BlockSpec((h, tm), lambda i: (0, i))
    return pl.pallas_call(
        _inproj_attn_t_kernel,
        grid=grid,
        in_specs=[row(D_MODEL), _full(n1.shape), _full(wqt.shape), _full(wiqt.shape), _full(wiwt.shape),
                  _full(wvt.shape), _full(wk.shape), _full(wv.shape), _full(wik.shape),
                  _full(lnw.shape), _full(lnb.shape)],
        out_specs=[col(ATTN_WIDTH), col(IDX_WIDTH), col(IDX_HEADS),
                   pl.BlockSpec((tm // Q_BLOCK, KV_WIDTH, Q_BLOCK), lambda i: (i, 0, 0)),
                   row(KV_WIDTH), row(KV_WIDTH), row(IDX_DIM)],
        out_shape=[jax.ShapeDtypeStruct((ATTN_WIDTH, m), BF16),
                   jax.ShapeDtypeStruct((IDX_WIDTH, m), BF16),
                   jax.ShapeDtypeStruct((IDX_HEADS, m), F32),
                   jax.ShapeDtypeStruct((m // Q_BLOCK, KV_WIDTH, Q_BLOCK), BF16),
                   jax.ShapeDtypeStruct((m, KV_WIDTH), F32),
                   jax.ShapeDtypeStruct((m, KV_WIDTH), F32),
                   jax.ShapeDtypeStruct((m, IDX_DIM), F32)],
        compiler_params=_params("arbitrary"),
        name="inproj_attn_prompt",
    )(x, n1, wqt, wiqt, wiwt, wvt, wk, wv, wik, lnw, lnb)


def _inproj_attn_r_kernel(x_ref, n1_ref, wq_ref, wk_ref, wv_ref, wiq_ref, wik_ref, wiw_ref,
                          lnw_ref, lnb_ref, q_ref, k_ref, v_ref, iq_ref, ik_ref, iw_ref):
    xn = _rms(x_ref[...], n1_ref[...]).astype(BF16)
    q_ref[...] = _dot(xn, wq_ref[...])
    k_ref[...] = _dot(xn, wk_ref[...])
    v_ref[...] = _dot(xn, wv_ref[...])
    iq_ref[...] = _dot(xn, wiq_ref[...]).astype(BF16)
    ik_ref[...] = _layernorm_rows(_dot(xn, wik_ref[...]), lnw_ref[...], lnb_ref[...])
    iw_ref[...] = _dot(xn, wiw_ref[...]) * IDX_SCALE


def _inproj_attn_r(x, n1, wq, wk, wv, wiq, wik, wiw, lnw, lnb):
    m = x.shape[0]
    ins = (x, n1, wq, wk, wv, wiq, wik, wiw, lnw, lnb)
    outs = [jax.ShapeDtypeStruct((m, ATTN_WIDTH), F32), jax.ShapeDtypeStruct((m, KV_WIDTH), F32),
            jax.ShapeDtypeStruct((m, KV_WIDTH), F32), jax.ShapeDtypeStruct((m, IDX_WIDTH), BF16),
            jax.ShapeDtypeStruct((m, IDX_DIM), F32), jax.ShapeDtypeStruct((m, LANES), F32)]
    return pl.pallas_call(
        _inproj_attn_r_kernel,
        grid=(1,),
        in_specs=[_full(a.shape) for a in ins],
        out_specs=[_full(o.shape) for o in outs],
        out_shape=outs,
        compiler_params=_params("arbitrary"),
        name="inproj_attn_sample",
    )(*ins)


def _inproj_ssm_kernel(x_ref, n1_ref, wz_ref, wx_ref, wdt_ref, z_ref, xbc_ref, dt_ref):
    xn = _rms(x_ref[...], n1_ref[...]).astype(BF16)
    z_ref[...] = _dot(xn, wz_ref[...])
    xbc_ref[...] = _dot(xn, wx_ref[...])
    dt_ref[...] = _dot(xn, wdt_ref[...])


def _inproj_ssm(x, n1, wz, wx, wdt, tm):
    m = x.shape[0]
    row = lambda w: pl.BlockSpec((tm, w), lambda i: (i, 0))
    return pl.pallas_call(
        _inproj_ssm_kernel,
        grid=(m // tm,),
        in_specs=[row(D_MODEL), _full(n1.shape), _full(wz.shape), _full(wx.shape), _full(wdt.shape)],
        out_specs=[row(D_INNER), row(CONV_DIM), row(LANES)],
        out_shape=[jax.ShapeDtypeStruct((m, D_INNER), F32),
                   jax.ShapeDtypeStruct((m, CONV_DIM), F32),
                   jax.ShapeDtypeStruct((m, LANES), F32)],
        compiler_params=_params("arbitrary"),
        name="inproj_ssm",
    )(x, n1, wz, wx, wdt)


def _prompt_attn_kernel(ik_ref, k_ref, vt_ref, qt_ref, iqt_ref, iwt_ref, o_ref, key_ref, *, n_sel):
    qb = pl.program_id(1)
    nkb = qb + 1
    row = lax.broadcasted_iota(I32, (Q_BLOCK, Q_BLOCK), 0)
    lane = lax.broadcasted_iota(I32, (Q_BLOCK, Q_BLOCK), 1)
    k_sel = float(n_sel)

    def score_block(kb, carry):
        off = pl.multiple_of(kb * Q_BLOCK, Q_BLOCK)
        ikb = ik_ref[pl.ds(off, Q_BLOCK), :].astype(BF16)
        sc = jnp.zeros((Q_BLOCK, Q_BLOCK), F32)
        for h in range(IDX_HEADS):
            r = _dot(ikb, iqt_ref[h * IDX_DIM:(h + 1) * IDX_DIM, :])
            sc = sc + jnp.maximum(r, 0.0) * iwt_ref[h:h + 1, :]
        key = _order_key(sc)
        reach = lane + jnp.where(kb < qb, Q_BLOCK, 0)
        key_ref[kb] = jnp.where(row <= reach, key, INT_MIN)
        return carry

    lax.fori_loop(0, nkb, score_block, 0)

    def count_ge(cand):
        def body(kb, acc):
            return acc + jnp.where(key_ref[kb] >= cand, 1.0, 0.0)
        acc = lax.fori_loop(0, nkb, body, jnp.zeros((Q_BLOCK, Q_BLOCK), F32))
        return jnp.sum(acc, axis=0, keepdims=True)

    zero = jnp.zeros((1, Q_BLOCK), I32)
    t0 = jnp.where(count_ge(zero) >= k_sel, zero, zero + INT_MIN)

    def bit_step(i, t):
        cand = t + jnp.left_shift(np.int32(1), 30 - i)
        return jnp.where(count_ge(cand) >= k_sel, cand, t)

    thr = lax.fori_loop(0, 31, bit_step, t0)
    thr = jnp.maximum(thr, INT_MIN + 1)
    n_ge = count_ge(thr)
    has_tie = jnp.max(n_ge) > k_sel

    @pl.when(jnp.logical_not(has_tie))
    def _():
        def body(kb, carry):
            key_ref[kb] = jnp.where(key_ref[kb] >= thr, 1, 0).astype(I32)
            return carry
        lax.fori_loop(0, nkb, body, 0)

    @pl.when(has_tie)
    def _():
        def count_gt():
            def body(kb, acc):
                return acc + jnp.where(key_ref[kb] > thr, 1.0, 0.0)
            acc = lax.fori_loop(0, nkb, body, jnp.zeros((Q_BLOCK, Q_BLOCK), F32))
            return jnp.sum(acc, axis=0, keepdims=True)
        need = k_sel - count_gt()
        below = jnp.where(lane < row, 1.0, 0.0).astype(BF16)

        def body(kb, seen):
            key = key_ref[kb]
            eq = key == thr
            eqf = jnp.where(eq, 1.0, 0.0)
            rank = _dot(below, eqf.astype(BF16)) + seen
            keep = jnp.logical_or(key > thr, jnp.logical_and(eq, rank < need))
            key_ref[kb] = jnp.where(keep, 1, 0).astype(I32)
            return seen + jnp.sum(eqf, axis=0, keepdims=True)
        lax.fori_loop(0, nkb, body, jnp.zeros((1, Q_BLOCK), F32))

    for h in range(N_HEADS):
        kv = h // Q_GROUP
        qth = qt_ref[h * HEAD_DIM:(h + 1) * HEAD_DIM, :]

        def attend(kb, carry, kv=kv, qth=qth):
            m, l, acc = carry
            off = pl.multiple_of(kb * Q_BLOCK, Q_BLOCK)
            kblk = k_ref[pl.ds(off, Q_BLOCK), kv * HEAD_DIM:(kv + 1) * HEAD_DIM].astype(BF16)
            lg = _dot(kblk, qth) * ATTN_SCALE
            sel = key_ref[kb] != 0
            m_new = jnp.maximum(m, jnp.max(jnp.where(sel, lg, NEG_BIG), axis=0, keepdims=True))
            alpha = jnp.exp(m - m_new)
            p = jnp.where(sel, jnp.exp(lg - m_new), 0.0)
            l_new = alpha * l + jnp.sum(p, axis=0, keepdims=True)
            vblk = vt_ref[kb, kv * HEAD_DIM:(kv + 1) * HEAD_DIM, :]
            acc_new = alpha * acc + _dot(vblk, p.astype(BF16))
            return m_new, l_new, acc_new

        init = (jnp.full((1, Q_BLOCK), NEG_BIG, F32), jnp.zeros((1, Q_BLOCK), F32),
                jnp.zeros((HEAD_DIM, Q_BLOCK), F32))
        _, l, acc = lax.fori_loop(0, nkb, attend, init)
        o_ref[:, h * HEAD_DIM:(h + 1) * HEAD_DIM] = (acc / l).T.astype(o_ref.dtype)


def _prompt_attn(ik, k, vt, qt, iqt, iwt, n_sel):
    b, s, _ = ik.shape
    nq = s // Q_BLOCK
    m = b * s
    colblk = lambda h: pl.BlockSpec((h, Q_BLOCK), lambda bi, qi: (0, bi * nq + qi))
    return pl.pallas_call(
        functools.partial(_prompt_attn_kernel, n_sel=n_sel),
        grid=(b, nq),
        in_specs=[pl.BlockSpec((None, s, IDX_DIM), lambda bi, qi: (bi, 0, 0)),
                  pl.BlockSpec((None, s, KV_WIDTH), lambda bi, qi: (bi, 0, 0)),
                  pl.BlockSpec((None, nq, KV_WIDTH, Q_BLOCK), lambda bi, qi: (bi, 0, 0, 0)),
                  colblk(ATTN_WIDTH), colblk(IDX_WIDTH), colblk(IDX_HEADS)],
        out_specs=pl.BlockSpec((Q_BLOCK, ATTN_WIDTH), lambda bi, qi: (bi * nq + qi, 0)),
        out_shape=jax.ShapeDtypeStruct((m, ATTN_WIDTH), BF16),
        scratch_shapes=[pltpu.VMEM((nq, Q_BLOCK, Q_BLOCK), I32)],
        compiler_params=_params("arbitrary", "arbitrary"),
        name="prompt_sparse_attn",
    )(ik, k, vt, qt, iqt, iwt)


def _gated_group_norm(y, z, nw):
    y = y * _silu(z)
    outs = []
    for g in range(SSM_GROUPS):
        yg = y[:, g * GROUP_WIDTH:(g + 1) * GROUP_WIDTH]
        outs.append(yg * lax.rsqrt(jnp.mean(yg * yg, axis=-1, keepdims=True) + EPS))
    return jnp.concatenate(outs, axis=-1) * nw


def _ssd_prompt_kernel(z_ref, xbc_ref, dt_ref, cw_ref, cb_ref, dtb_ref, a_ref, dsk_ref, nw_ref, rep_ref,
                       y_ref, h_ref, xpad_ref, ht_ref):
    c = pl.program_id(1)
    pad0 = SUBLANES - (CONV_W - 1)

    @pl.when(c == 0)
    def _():
        ht_ref[...] = jnp.zeros_like(ht_ref)
        xpad_ref[0:SUBLANES, :] = jnp.zeros((SUBLANES, CONV_DIM), F32)

    xpad_ref[SUBLANES:SUBLANES + CHUNK, :] = xbc_ref[...]
    acc = cb_ref[...]
    for i in range(CONV_W):
        acc = acc + xpad_ref[pad0 + i:pad0 + i + CHUNK, :] * cw_ref[i:i + 1, :]
    act = _silu(acc)
    xpad_ref[pad0:SUBLANES, :] = xpad_ref[CHUNK + pad0:CHUNK + SUBLANES, :]

    xs = act[:, :D_INNER]
    bm = act[:, D_INNER:D_INNER + SSM_GROUPS * D_STATE]
    cm = act[:, D_INNER + SSM_GROUPS * D_STATE:]
    rep = rep_ref[...]

    row = lax.broadcasted_iota(I32, (CHUNK, CHUNK), 0)
    col = lax.broadcasted_iota(I32, (CHUNK, CHUNK), 1)
    tri = row >= col
    dt = _softplus(dt_ref[...] + dtb_ref[...])
    a_cum = _dot3_left(jnp.where(tri, 1.0, 0.0).astype(BF16), dt * a_ref[...])
    a_cum_t = a_cum.T
    a_last = a_cum[CHUNK - 1:CHUNK, :]
    dt_rep = _dot3(dt, rep)
    ea_rep = _dot3(jnp.exp(a_cum), rep)
    w_rep = _dot3(jnp.exp(a_last - a_cum), rep)
    ea_last_rep = _dot3(jnp.broadcast_to(jnp.exp(a_last), (SUBLANES, LANES)), rep)[0:1, :]

    xdt = xs * dt_rep
    xdt_b = xdt.astype(BF16)
    xw_b = (xdt * w_rep).astype(BF16)
    lane_d = lax.broadcasted_iota(I32, (CHUNK, 2 * SSM_HEAD_DIM), 1)
    left = lane_d < SSM_HEAD_DIM

    y_parts = []
    for g in range(SSM_GROUPS):
        bg = bm[:, g * D_STATE:(g + 1) * D_STATE].astype(BF16)
        cg = cm[:, g * D_STATE:(g + 1) * D_STATE].astype(BF16)
        cb = _dot_nt(cg, bg)
        cols = slice(g * GROUP_WIDTH, (g + 1) * GROUP_WIDTH)
        y_off = _dot(cg, ht_ref[:, cols].astype(BF16)) * ea_rep[:, cols]
        for j in range(HEADS_PER_GROUP // 2):
            ws = []
            for r in (g * HEADS_PER_GROUP + 2 * j, g * HEADS_PER_GROUP + 2 * j + 1):
                seg = a_cum[:, r:r + 1] - a_cum_t[r:r + 1, :]
                ws.append((cb * jnp.where(tri, jnp.exp(seg), 0.0)).astype(BF16))
            c0 = g * GROUP_WIDTH + 2 * j * SSM_HEAD_DIM
            xp = xdt_b[:, c0:c0 + 2 * SSM_HEAD_DIM]
            zeros = jnp.zeros_like(xp)
            xbd = jnp.concatenate([jnp.where(left, xp, zeros), jnp.where(left, zeros, xp)], axis=0)
            y_parts.append(_dot(jnp.concatenate(ws, axis=1), xbd)
                           + y_off[:, 2 * j * SSM_HEAD_DIM:2 * (j + 1) * SSM_HEAD_DIM])
        upd = _dot_tn(bg, xw_b[:, cols])
        ht_ref[:, cols] = ht_ref[:, cols] * ea_last_rep[:, cols] + upd

    y = jnp.concatenate(y_parts, axis=-1) + dsk_ref[...] * xs
    y_ref[...] = _gated_group_norm(y, z_ref[...], nw_ref[...]).astype(y_ref.dtype)

    @pl.when(c == pl.num_programs(1) - 1)
    def _():
        h_ref[...] = ht_ref[...].T


def _ssd_prompt(z, xbc, dt, cw, cb, dtb, a_neg, dsk, nw, rep, b, s):
    nc = s // CHUNK
    rowblk = lambda w: pl.BlockSpec((CHUNK, w), lambda bi, ci: (bi * nc + ci, 0))
    ins = (cw, cb, dtb, a_neg, dsk, nw, rep)
    return pl.pallas_call(
        _ssd_prompt_kernel,
        grid=(b, nc),
        in_specs=[rowblk(D_INNER), rowblk(CONV_DIM), rowblk(LANES)] + [_full(a.shape) for a in ins],
        out_specs=[rowblk(D_INNER),
                   pl.BlockSpec((None, D_INNER, D_STATE), lambda bi, ci: (bi, 0, 0))],
        out_shape=[jax.ShapeDtypeStruct((b * s, D_INNER), BF16),
                   jax.ShapeDtypeStruct((b, D_INNER, D_STATE), F32)],
        scratch_shapes=[pltpu.VMEM((CHUNK + SUBLANES, CONV_DIM), F32),
                        pltpu.VMEM((D_STATE, D_INNER), F32)],
        compiler_params=_params("arbitrary", "arbitrary"),
        name="ssd_prompt",
    )(z, xbc, dt, *ins)


def _merge_kernel(x_ref, ao_ref, so_ref, n1_ref, wga_ref, wgm_ref, wab_ref, wsb_ref, wo_ref, x1_ref):
    x = x_ref[...]
    xn = _rms(x, n1_ref[...]).astype(BF16)
    ga = _dot(xn, wga_ref[...])
    gm = _dot(xn, wgm_ref[...])
    mix = (_sigmoid(ga) * _dot(ao_ref[...].astype(BF16), wab_ref[...])
           + _sigmoid(gm) * _dot(so_ref[...].astype(BF16), wsb_ref[...]))
    x1_ref[...] = x + _dot(mix.astype(BF16), wo_ref[...])


def _merge(x, ao, so, n1, wga, wgm, wab, wsb, wo, tm):
    m = x.shape[0]
    row = lambda w: pl.BlockSpec((tm, w), lambda i: (i, 0))
    ws = (n1, wga, wgm, wab, wsb, wo)
    return pl.pallas_call(
        _merge_kernel,
        grid=(m // tm,),
        in_specs=[row(D_MODEL), row(ATTN_WIDTH), row(D_INNER)] + [_full(w.shape) for w in ws],
        out_specs=row(D_MODEL),
        out_shape=jax.ShapeDtypeStruct((m, D_MODEL), F32),
        compiler_params=_params("arbitrary"),
        name="merge",
    )(x, ao, so, *ws)


def _ffn_kernel(x1_ref, n2_ref, wg_ref, wu_ref, wd_ref, nf_ref, y_ref):
    x1 = x1_ref[...]
    h = _rms(x1, n2_ref[...]).astype(BF16)
    act = _silu(_dot(h, wg_ref[...])) * _dot(h, wu_ref[...])
    x2 = x1 + _dot(act.astype(BF16), wd_ref[...])
    y_ref[...] = _rms(x2, nf_ref[...])


def _ffn(x1, n2, wg, wu, wd, nf, tm):
    m = x1.shape[0]
    row = pl.BlockSpec((tm, D_MODEL), lambda i: (i, 0))
    ws = (n2, wg, wu, wd, nf)
    return pl.pallas_call(
        _ffn_kernel,
        grid=(m // tm,),
        in_specs=[row] + [_full(w.shape) for w in ws],
        out_specs=row,
        out_shape=jax.ShapeDtypeStruct((m, D_MODEL), F32),
        compiler_params=_params("arbitrary"),
        name="ffn",
    )(x1, *ws)


PAGES_PER_STEP = 8


def _sample_scores_kernel(pt_ref, iq_ref, iw_ref, ikn_ref, *rest):
    pages = rest[:PAGES_PER_STEP]
    sc_ref, scn_ref = rest[PAGES_PER_STEP:]
    iq = iq_ref[...]
    iw = iw_ref[...]

    def score(keys):
        r = _dot_nt(iq, keys.astype(BF16))
        return jnp.sum(jnp.maximum(r, 0.0) * iw, axis=0, keepdims=True)

    for i, page in enumerate(pages):
        sc_ref[:, i * PAGE_SIZE:(i + 1) * PAGE_SIZE] = score(page[...])
    scn_ref[...] = score(jnp.broadcast_to(ikn_ref[...], (LANES, IDX_DIM)))


def _sample_scores(page_table, iq3, iw3, ikn3, cache_idx):
    db, n_pages = page_table.shape
    steps = n_pages // PAGES_PER_STEP
    page_spec = lambda i: pl.BlockSpec(
        (None, PAGE_SIZE, IDX_DIM), lambda b, j, pt: (pt[b, j * PAGES_PER_STEP + i], 0, 0))
    grid_spec = pltpu.PrefetchScalarGridSpec(
        num_scalar_prefetch=1,
        grid=(db, steps),
        in_specs=[pl.BlockSpec((None, IDX_HEADS, IDX_DIM), lambda b, j, pt: (b, 0, 0)),
                  pl.BlockSpec((None, IDX_HEADS, 1), lambda b, j, pt: (b, 0, 0)),
                  pl.BlockSpec((None, 1, IDX_DIM), lambda b, j, pt: (b, 0, 0))]
                 + [page_spec(i) for i in range(PAGES_PER_STEP)],
        out_specs=[pl.BlockSpec((None, 1, PAGES_PER_STEP * PAGE_SIZE), lambda b, j, pt: (b, 0, j)),
                   pl.BlockSpec((None, 1, LANES), lambda b, j, pt: (b, 0, 0))],
    )
    return pl.pallas_call(
        _sample_scores_kernel,
        grid_spec=grid_spec,
        out_shape=[jax.ShapeDtypeStruct((db, 1, n_pages * PAGE_SIZE), F32),
                   jax.ShapeDtypeStruct((db, 1, LANES), F32)],
        compiler_params=_params("arbitrary", "arbitrary"),
        name="sample_index_scores",
    )(page_table, iq3, iw3, ikn3, *([cache_idx] * PAGES_PER_STEP))


def _sample_select_kernel(sc_ref, scn_ref, sel2_ref, seln_ref, *, n_sel):
    db, past = sc_ref.shape
    k_sel = float(n_sel)
    key = _order_key(sc_ref[...])
    keyn = _order_key(scn_ref[...])[:, 0:1]

    def count(pred_past, pred_new):
        return (jnp.sum(jnp.where(pred_past, 1.0, 0.0), axis=1, keepdims=True)
                + jnp.where(pred_new, 1.0, 0.0))

    def count_ge(cand):
        return count(key >= cand, keyn >= cand)

    zero = jnp.zeros((db, 1), I32)
    t0 = jnp.where(count_ge(zero) >= k_sel, zero, zero + INT_MIN)

    def bit_step(i, t):
        cand = t + jnp.left_shift(np.int32(1), 30 - i)
        return jnp.where(count_ge(cand) >= k_sel, cand, t)

    thr = lax.fori_loop(0, 31, bit_step, t0)
    need = k_sel - count(key > thr, keyn > thr)

    r_i = lax.broadcasted_iota(I32, (LANES, LANES), 0)
    c_i = lax.broadcasted_iota(I32, (LANES, LANES), 1)
    before = jnp.where(r_i < c_i, 1.0, 0.0).astype(BF16)
    dup = jnp.where(jnp.right_shift(lax.broadcasted_iota(I32, (LANES, 2 * LANES), 1), 1)
                    == lax.broadcasted_iota(I32, (LANES, 2 * LANES), 0), 1.0, 0.0).astype(BF16)
    seen = jnp.zeros((db, 1), F32)
    for j in range(past // LANES):
        kj = key[:, j * LANES:(j + 1) * LANES]
        eq = kj == thr
        eqf = jnp.where(eq, 1.0, 0.0)
        rank = _dot(eqf.astype(BF16), before) + seen
        keep = jnp.logical_or(kj > thr, jnp.logical_and(eq, rank < need))
        keepf = jnp.where(keep, 1.0, 0.0).astype(BF16)
        sel2_ref[:, 2 * j * LANES:2 * (j + 1) * LANES] = _dot(keepf, dup)
        seen = seen + jnp.sum(eqf, axis=1, keepdims=True)
    keepn = jnp.logical_or(keyn > thr, jnp.logical_and(keyn == thr, seen < need))
    seln_ref[...] = jnp.broadcast_to(jnp.where(keepn, 1.0, 0.0), seln_ref.shape)


def _sample_select(sc, scn, n_sel):
    db, past = sc.shape
    outs = [jax.ShapeDtypeStruct((db, 2 * past), F32), jax.ShapeDtypeStruct((db, LANES), F32)]
    return pl.pallas_call(
        functools.partial(_sample_select_kernel, n_sel=n_sel),
        grid=(1,),
        in_specs=[_full(sc.shape), _full(scn.shape)],
        out_specs=[_full(o.shape) for o in outs],
        out_shape=outs,
        compiler_params=_params("arbitrary"),
        name="sample_select",
    )(sc, scn)


def _sample_attn_kernel(pt_ref, q_ref, sel2_ref, seln_ref, kn_ref, vn_ref, *rest):
    kpages = rest[:PAGES_PER_STEP]
    vpages = rest[PAGES_PER_STEP:2 * PAGES_PER_STEP]
    o_ref, m_ref, l_ref, acc_ref = rest[2 * PAGES_PER_STEP:]
    j = pl.program_id(1)
    rows2 = 2 * PAGE_SIZE
    width = PAGES_PER_STEP * rows2

    @pl.when(j == 0)
    def _():
        m_ref[...] = jnp.full(m_ref.shape, NEG_BIG, F32)
        l_ref[...] = jnp.zeros(l_ref.shape, F32)
        acc_ref[...] = jnp.zeros(acc_ref.shape, F32)

    q = q_ref[...]
    qb = q.astype(BF16)
    lg = jnp.concatenate([_dot_nt(qb, kp[...].astype(BF16)) for kp in kpages], axis=1) * ATTN_SCALE
    head_kv = lax.broadcasted_iota(I32, (N_HEADS, width), 0) // Q_GROUP
    col_kv = jnp.bitwise_and(lax.broadcasted_iota(I32, (N_HEADS, width), 1), 1)
    sel = jnp.logical_and(sel2_ref[...] != 0.0, head_kv == col_kv)
    m = m_ref[...]
    m_new = jnp.maximum(m, jnp.max(jnp.where(sel, lg, NEG_BIG), axis=1, keepdims=True))
    alpha = jnp.exp(m - m_new)
    p = jnp.where(sel, jnp.exp(lg - m_new), 0.0)
    l_ref[...] = alpha * l_ref[...] + jnp.sum(p, axis=1, keepdims=True)
    pb = p.astype(BF16)
    pv = jnp.zeros((N_HEADS, HEAD_DIM), F32)
    for i, vp in enumerate(vpages):
        pv = pv + _dot(pb[:, i * rows2:(i + 1) * rows2], vp[...].astype(BF16))
    acc_ref[...] = alpha * acc_ref[...] + pv
    m_ref[...] = m_new

    @pl.when(j == pl.num_programs(1) - 1)
    def _():
        kn = kn_ref[...]
        vn = vn_ref[...]
        hk = lax.broadcasted_iota(I32, (N_HEADS, HEAD_DIM), 0) // Q_GROUP
        kexp = jnp.where(hk == 0, kn[0:1, :], kn[1:2, :])
        vexp = jnp.where(hk == 0, vn[0:1, :], vn[1:2, :])
        lgn = jnp.sum(q * kexp, axis=1, keepdims=True) * ATTN_SCALE
        seln = seln_ref[...][:, 0:1] != 0.0
        m1 = m_ref[...]
        m2 = jnp.maximum(m1, jnp.where(seln, lgn, NEG_BIG))
        a2 = jnp.exp(m1 - m2)
        pn = jnp.where(seln, jnp.exp(lgn - m2), 0.0)
        l2 = a2 * l_ref[...] + pn
        acc2 = a2 * acc_ref[...] + pn * vexp
        o_ref[...] = acc2 / l2


def _sample_attn(page_table, q3, sel2, seln, kn3, vn3, cache_k2, cache_v2):
    db, n_pages = page_table.shape
    steps = n_pages // PAGES_PER_STEP
    rows2 = 2 * PAGE_SIZE
    page_spec = lambda i: pl.BlockSpec(
        (None, rows2, HEAD_DIM), lambda b, j, pt: (pt[b, j * PAGES_PER_STEP + i], 0, 0))
    per_b = lambda shape: pl.BlockSpec((None,) + shape, lambda b, j, pt: (b, 0, 0))
    grid_spec = pltpu.PrefetchScalarGridSpec(
        num_scalar_prefetch=1,
        grid=(db, steps),
        in_specs=[per_b((N_HEADS, HEAD_DIM)),
                  pl.BlockSpec((None, 1, PAGES_PER_STEP * rows2), lambda b, j, pt: (b, 0, j)),
                  per_b((1, LANES)), per_b((N_KV_HEADS, HEAD_DIM)), per_b((N_KV_HEADS, HEAD_DIM))]
                 + [page_spec(i) for i in range(PAGES_PER_STEP)] * 2,
        out_specs=per_b((N_HEADS, HEAD_DIM)),
        scratch_shapes=[pltpu.VMEM((N_HEADS, 1), F32), pltpu.VMEM((N_HEADS, 1), F32),
                        pltpu.VMEM((N_HEADS, HEAD_DIM), F32)],
    )
    return pl.pallas_call(
        _sample_attn_kernel,
        grid_spec=grid_spec,
        out_shape=jax.ShapeDtypeStruct((db, N_HEADS, HEAD_DIM), F32),
        compiler_params=_params("arbitrary", "arbitrary"),
        name="sample_sparse_attn",
    )(page_table, q3, sel2, seln, kn3, vn3,
      *([cache_k2] * PAGES_PER_STEP), *([cache_v2] * PAGES_PER_STEP))


def _ssd_sample_kernel(h_ref, sc_ref, xn_ref, z_ref, dt_ref, cw_ref, cb_ref, dtb_ref, a_ref, dsk_ref,
                       nw_ref, rep_ref, y_ref, hn_ref):
    cw = cw_ref[...]
    sc = sc_ref[...]
    acc = cb_ref[...]
    for i in range(CONV_W - 1):
        acc = acc + sc[i:i + 1, :] * cw[i:i + 1, :]
    acc = acc + xn_ref[...] * cw[CONV_W - 1:CONV_W, :]
    act = _silu(acc)
    xs = act[:, :D_INNER]
    bm = act[:, D_INNER:D_INNER + SSM_GROUPS * D_STATE]
    cm = act[:, D_INNER + SSM_GROUPS * D_STATE:]

    rep = rep_ref[...]
    dt = jnp.broadcast_to(_softplus(dt_ref[...] + dtb_ref[...]), (SUBLANES, LANES))
    dt_rep = _dot3(dt, rep)[0:1, :]
    decay_rep = _dot3(jnp.exp(dt * a_ref[...]), rep)[0:1, :]
    xdt = xs * dt_rep

    half = lax.broadcasted_iota(I32, (LANES, D_INNER), 0) < LANES // 2
    slab_t = jnp.where(half, jnp.broadcast_to(xdt, (LANES, D_INNER)),
                       jnp.broadcast_to(decay_rep, (LANES, D_INNER))).T
    xdt_col = slab_t[:, 0:1]
    decay_col = slab_t[:, LANES // 2:LANES // 2 + 1]

    y_parts = []
    for g in range(SSM_GROUPS):
        rows = slice(g * GROUP_WIDTH, (g + 1) * GROUP_WIDTH)
        bg = bm[:, g * D_STATE:(g + 1) * D_STATE]
        cg = cm[:, g * D_STATE:(g + 1) * D_STATE]
        hg = h_ref[rows, :] * decay_col[rows, :] + xdt_col[rows, :] * bg
        hn_ref[rows, :] = hg
        y_parts.append(jnp.sum(hg * cg, axis=1, keepdims=True))
    y_col = jnp.concatenate(y_parts, axis=0)
    y = jnp.broadcast_to(y_col, (D_INNER, LANES)).T[0:1, :]
    y = y + dsk_ref[...] * xs
    y_ref[...] = _gated_group_norm(y, z_ref[...], nw_ref[...]).astype(y_ref.dtype)


def _ssd_sample(h, sconv, xbc3, z3, dt3, cw, cb, dtb, a_neg, dsk, nw, rep):
    db = h.shape[0]
    per_b = lambda shape: pl.BlockSpec((None,) + shape, lambda b: (b,) + (0,) * len(shape))
    ws = (cw, cb, dtb, a_neg, dsk, nw, rep)
    return pl.pallas_call(
        _ssd_sample_kernel,
        grid=(db,),
        in_specs=[per_b((D_INNER, D_STATE)), per_b((CONV_W - 1, CONV_DIM)), per_b((1, CONV_DIM)),
                  per_b((1, D_INNER)), per_b((1, LANES))] + [_full(w.shape) for w in ws],
        out_specs=[per_b((1, D_INNER)), per_b((D_INNER, D_STATE))],
        out_shape=[jax.ShapeDtypeStruct((db, 1, D_INNER), BF16),
                   jax.ShapeDtypeStruct((db, D_INNER, D_STATE), F32)],
        compiler_params=_params("arbitrary"),
        name="ssd_sample",
    )(h, sconv, xbc3, z3, dt3, *ws)


def _topk_count(n_keys):
    return min(TOPK_MAX, n_keys // 4)


def _pad_cols(w, width):
    return jnp.pad(w, ((0, 0), (0, width - w.shape[1])))


def kernel(x_prompt, x_sample, cache_k, cache_v, cache_idx_k, state_ssm, state_conv, page_table,
           norm1_w, w_in, idx_ln_w, idx_ln_b, conv_w, conv_b, dt_bias, a_log, d_skip, ssm_norm_w,
           w_attn_br, w_ssm_br, w_out, norm2_w, w_ffn_gate, w_ffn_up, w_ffn_down, normf_w):
    depth = w_in.shape[0]
    assert depth == 1, "single-layer stack"
    b, s, _ = x_prompt.shape
    db, ds, _ = x_sample.shape
    assert ds == 1
    n_phys = cache_k.shape[1]
    past = page_table.shape[1] * PAGE_SIZE

    wi = w_in[0]
    seg = lambda a, z: wi[:, a:z].astype(BF16)
    wq, wk, wv, wiq, wik = seg(O_Q, O_K), seg(O_K, O_V), seg(O_V, O_IQ), seg(O_IQ, O_IK), seg(O_IK, O_IW)
    wiw = seg(O_IW, O_Z)
    wz, wx = seg(O_Z, O_XBC), seg(O_XBC, O_DT)
    wdt = _pad_cols(seg(O_DT, O_GA), LANES)
    wga, wgm = seg(O_GA, O_GM), seg(O_GM, O_END)
    row2 = lambda v: v.reshape(1, -1).astype(F32)
    n1, n2, nf = row2(norm1_w[0]), row2(norm2_w[0]), row2(normf_w)
    lnw, lnb = row2(idx_ln_w[0]), row2(idx_ln_b[0])
    cw, cb = conv_w[0].astype(F32), row2(conv_b[0])
    dtb = _pad_cols(row2(dt_bias[0]), LANES)
    a_neg = _pad_cols(row2(-jnp.exp(a_log[0].astype(F32))), LANES)
    dsk = row2(jnp.repeat(d_skip[0].astype(F32), SSM_HEAD_DIM))
    nw = row2(ssm_norm_w[0])
    rep = (jnp.arange(LANES)[:, None] == (jnp.arange(D_INNER)[None, :] // SSM_HEAD_DIM)).astype(BF16)
    wab, wsb, wo = w_attn_br[0].astype(BF16), w_ssm_br[0].astype(BF16), w_out[0].astype(BF16)
    wg, wu, wd = w_ffn_gate[0].astype(BF16), w_ffn_up[0].astype(BF16), w_ffn_down[0].astype(BF16)

    xp = x_prompt.reshape(b * s, D_MODEL)
    tm = 256
    qt, iqt, iwt, vt, k_p, v_p, ik_p = _inproj_attn_t(
        xp, n1, wq.T, wiq.T, wiw.T, wv.T, wk, wv, wik, lnw, lnb, tm)
    z_p, xbc_p, dt_p = _inproj_ssm(xp, n1, wz, wx, wdt, tm)
    ao_p = _prompt_attn(ik_p.reshape(b, s, IDX_DIM), k_p.reshape(b, s, KV_WIDTH),
                        vt.reshape(b, s // Q_BLOCK, KV_WIDTH, Q_BLOCK), qt, iqt, iwt, _topk_count(s))
    so_p, h_p = _ssd_prompt(z_p, xbc_p, dt_p, cw, cb, dtb, a_neg, dsk, nw, rep, b, s)
    x1_p = _merge(xp, ao_p, so_p, n1, wga, wgm, wab, wsb, wo, tm)
    y_p = _ffn(x1_p, n2, wg, wu, wd, nf, tm)

    xs_ = x_sample.reshape(db, D_MODEL)
    q_s, k_s, v_s, iq_s, ik_s, iw_s = _inproj_attn_r(xs_, n1, wq, wk, wv, wiq, wik, _pad_cols(wiw, LANES),
                                                      lnw, lnb)
    z_s, xbc_s, dt_s = _inproj_ssm(xs_, n1, wz, wx, wdt, db)
    sc, scn = _sample_scores(page_table, iq_s.reshape(db, IDX_HEADS, IDX_DIM),
                             iw_s[:, :IDX_HEADS].reshape(db, IDX_HEADS, 1),
                             ik_s.reshape(db, 1, IDX_DIM),
                             cache_idx_k.reshape(n_phys, PAGE_SIZE, IDX_DIM))
    sel2, seln = _sample_select(sc.reshape(db, past), scn.reshape(db, LANES), _topk_count(past + ds))
    ao_s = _sample_attn(page_table, q_s.reshape(db, N_HEADS, HEAD_DIM), sel2.reshape(db, 1, 2 * past),
                        seln.reshape(db, 1, LANES), k_s.reshape(db, N_KV_HEADS, HEAD_DIM),
                        v_s.reshape(db, N_KV_HEADS, HEAD_DIM),
                        cache_k.reshape(n_phys, 2 * PAGE_SIZE, HEAD_DIM),
                        cache_v.reshape(n_phys, 2 * PAGE_SIZE, HEAD_DIM))
    so_s, h_s = _ssd_sample(state_ssm.reshape(db, D_INNER, D_STATE), state_conv[0],
                            xbc_s.reshape(db, 1, CONV_DIM), z_s.reshape(db, 1, D_INNER),
                            dt_s.reshape(db, 1, LANES), cw, cb, dtb, a_neg, dsk, nw, rep)
    x1_s = _merge(xs_, ao_s.reshape(db, ATTN_WIDTH), so_s.reshape(db, D_INNER), n1, wga, wgm, wab, wsb, wo, db)
    y_s = _ffn(x1_s, n2, wg, wu, wd, nf, db)

    conv_p = xbc_p.reshape(b, s, CONV_DIM)[:, s - (CONV_W - 1):]
    conv_s = jnp.concatenate([state_conv[0][:, 1:], xbc_s.reshape(db, 1, CONV_DIM)], axis=1)
    return (y_p.reshape(b, s, D_MODEL), y_s.reshape(db, ds, D_MODEL),
            k_p.reshape(1, b, s, N_KV_HEADS, HEAD_DIM), v_p.reshape(1, b, s, N_KV_HEADS, HEAD_DIM),
            ik_p.reshape(1, b, s, IDX_DIM), h_p.reshape(1, b, SSM_HEADS, SSM_HEAD_DIM, D_STATE),
            conv_p[None],
            k_s.reshape(1, db, ds, N_KV_HEADS, HEAD_DIM), v_s.reshape(1, db, ds, N_KV_HEADS, HEAD_DIM),
            ik_s.reshape(1, db, ds, IDX_DIM), h_s.reshape(1, db, SSM_HEADS, SSM_HEAD_DIM, D_STATE),
            conv_s[None])
```
